```python
import math
import jax, jax.numpy as jnp
from jax import lax
import numpy as np

D_MODEL = 1024
BATCH = 32
SEQ = 2048
DEPTH = 1

D_MIX = 2 * D_MODEL
SSD_WIDTH = D_MIX // 2
SSD_HEAD_DIM = 64
SSD_HEADS = SSD_WIDTH // SSD_HEAD_DIM
SSD_GROUPS = 2
SSD_STATE = 128
CONV_WIDTH = 4
CONV_DIM = SSD_WIDTH + 2 * SSD_GROUPS * SSD_STATE
RET_WIDTH = D_MIX - SSD_WIDTH
RET_HEADS = 4
RET_HEAD_DIM = RET_WIDTH // RET_HEADS
CHUNK = 128
IN_COLS = SSD_WIDTH + CONV_DIM + SSD_HEADS + 4 * RET_WIDTH
N_MEM = 256
XATTN_HEADS = 4
XATTN_HEAD_DIM = D_MODEL // XATTN_HEADS
D_FF = ((8 * D_MODEL // 3 + 255) // 256) * 256
ROPE_BASE = 10000.0
EPS = 1e-6
POS_OFFSET_MAX = 1024

kernel_name = "hybrid_ssd_retention_xattn_block"


def _rms(x, eps=EPS):
    xf = x.astype(jnp.float32)
    return xf * lax.rsqrt(jnp.mean(xf * xf, axis=-1, keepdims=True) + eps)


def _rmsnorm(x, g):
    return (_rms(x) * g.astype(jnp.float32)).astype(x.dtype)


def _causal_dwconv(u, w, b):
    c = u.shape[-1]
    y = lax.conv_general_dilated(
        u, w[:, None, :].astype(u.dtype), window_strides=(1,),
        padding=[(CONV_WIDTH - 1, 0)], dimension_numbers=("NWC", "WIO", "NWC"),
        feature_group_count=c)
    return y + b.astype(u.dtype)


def _ssd_scan(x_dt, da, bmat, cmat):
    bsz, L, H, P = x_dt.shape
    nc = L // CHUNK
    r = H // SSD_GROUPS
    to_chunks = lambda t, shp: jnp.moveaxis(t.reshape((bsz, nc, CHUNK) + shp), 1, 0)
    xc_all = to_chunks(x_dt, (SSD_GROUPS, r, P))
    da_all = to_chunks(da, (SSD_GROUPS, r))
    b_all = to_chunks(bmat, (SSD_GROUPS, SSD_STATE))
    c_all = to_chunks(cmat, (SSD_GROUPS, SSD_STATE))
    causal = jnp.tril(jnp.ones((CHUNK, CHUNK), dtype=bool))[None, :, :, None, None]

    def step(state, inp):
        xc, dac, bc, cc = inp
        acs = jnp.cumsum(dac, axis=1)
        seg = acs[:, :, None] - acs[:, None, :]
        decay = jnp.exp(jnp.where(causal, seg, -jnp.inf))
        cb = jnp.einsum("btgn,bsgn->btsg", cc, bc)
        y_diag = jnp.einsum("btsg,btsgr,bsgrp->btgrp", cb, decay, xc)
        y_off = jnp.einsum("btgn,bgrpn,btgr->btgrp", cc, state, jnp.exp(acs))
        last = acs[:, -1]
        w_s = jnp.exp(last[:, None] - acs)
        new_state = state * jnp.exp(last)[..., None, None] + jnp.einsum(
            "bsgn,bsgr,bsgrp->bgrpn", bc, w_s, xc)
        return new_state, y_diag + y_off

    state0 = jnp.zeros((bsz, SSD_GROUPS, r, P, SSD_STATE), jnp.float32)
    _, y = lax.scan(step, state0, (xc_all, da_all, b_all, c_all))
    return jnp.moveaxis(y, 0, 1).reshape(bsz, L, H, P)


def _rotate(t, pos):
    half = t.shape[-1] // 2
    inv_freq = 1.0 / (ROPE_BASE ** jnp.linspace(0.0, 1.0, half, dtype=jnp.float32))
    ang = pos[:, :, None, None] * inv_freq
    cos, sin = jnp.cos(ang), jnp.sin(ang)
    t0, t1 = t[..., 0::2], t[..., 1::2]
    return jnp.stack([t0 * cos - t1 * sin, t1 * cos + t0 * sin], axis=-1).reshape(t.shape)


def _retention_scan(q, k, v):
    bsz, L, H, dk = q.shape
    dv = v.shape[-1]
    nc = L // CHUNK
    log_gamma = jnp.log(1.0 - 2.0 ** (-5.0 - jnp.arange(H, dtype=jnp.float32)))
    idx = jnp.arange(CHUNK, dtype=jnp.float32)
    rel = idx[:, None] - idx[None, :]
    intra = jnp.where(rel[None] >= 0,
                      jnp.exp(jnp.maximum(rel, 0.0)[None] * log_gamma[:, None, None]), 0.0)
    xi = jnp.exp((idx + 1.0)[None] * log_gamma[:, None])
    zeta = jnp.exp((CHUNK - 1.0 - idx)[None] * log_gamma[:, None])
    chunk_decay = jnp.exp(CHUNK * log_gamma)
    to_chunks = lambda t: t.reshape(bsz, nc, CHUNK, H, -1).transpose(1, 0, 3, 2, 4)

    def step(state, inp):
        qc, kc, vc = inp
        scores = jnp.einsum("bhtd,bhsd->bhts", qc, kc) * intra
        out = jnp.einsum("bhts,bhse->bhte", scores, vc) + \
            jnp.einsum("bhtd,bhde->bhte", qc, state) * xi[..., None]
        new_state = state * chunk_decay[:, None, None] + \
            jnp.einsum("bhsd,hs,bhse->bhde", kc, zeta, vc)
        return new_state, out

    state0 = jnp.zeros((bsz, H, dk, dv), jnp.float32)
    _, o = lax.scan(step, state0, (to_chunks(q), to_chunks(k), to_chunks(v)))
    return o.transpose(1, 0, 3, 2, 4).reshape(bsz, L, H, dv)


def _hybrid_mixer(hn, pos, w_in, conv_w, conv_b, dt_bias, a_log, d_skip, ssd_norm_g, w_out):
    bsz, L, _ = hn.shape
    proj = jnp.einsum("bld,de->ble", hn, w_in)
    o0 = SSD_WIDTH
    o1 = o0 + CONV_DIM
    o2 = o1 + SSD_HEADS
    z = proj[..., :o0]
    xbc = proj[..., o0:o1]
    dt_raw = proj[..., o1:o2]
    q, k, v, g = jnp.split(proj[..., o2:], 4, axis=-1)

    xbc = jax.nn.silu(_causal_dwconv(xbc, conv_w, conv_b)).astype(jnp.float32)
    xs = xbc[..., :SSD_WIDTH].reshape(bsz, L, SSD_HEADS, SSD_HEAD_DIM)
    bmat = xbc[..., SSD_WIDTH:SSD_WIDTH + SSD_GROUPS * SSD_STATE].reshape(bsz, L, SSD_GROUPS, SSD_STATE)
    cmat = xbc[..., SSD_WIDTH + SSD_GROUPS * SSD_STATE:].reshape(bsz, L, SSD_GROUPS, SSD_STATE)
    dt = jax.nn.softplus(dt_raw.astype(jnp.float32) + dt_bias.astype(jnp.float32))
    a = -jnp.exp(a_log.astype(jnp.float32))
    y = _ssd_scan(xs * dt[..., None], dt * a, bmat, cmat)
    y = y + d_skip.astype(jnp.float32)[:, None] * xs
    y = y.reshape(bsz, L, SSD_WIDTH) * jax.nn.silu(z.astype(jnp.float32))
    y_ssd = (_rms(y) * ssd_norm_g.astype(jnp.float32)).astype(hn.dtype)

    qr = _rotate(q.astype(jnp.float32).reshape(bsz, L, RET_HEADS, RET_HEAD_DIM), pos)
    kr = _rotate(k.astype(jnp.float32).reshape(bsz, L, RET_HEADS, RET_HEAD_DIM), pos) * (RET_HEAD_DIM ** -0.5)
    vr = v.astype(jnp.float32).reshape(bsz, L, RET_HEADS, RET_HEAD_DIM)
    o = _retention_scan(qr, kr, vr)
    o = _rms(o).reshape(bsz, L, RET_WIDTH)
    y_ret = (jax.nn.silu(g.astype(jnp.float32)) * o).astype(hn.dtype)

    return jnp.einsum("ble,ed->bld", jnp.concatenate([y_ssd, y_ret], axis=-1), w_out)


def _cross_attn(hn, mem, wq, wk, wv, wo):
    bsz, L, _ = hn.shape
    q = (hn @ wq).reshape(bsz, L, XATTN_HEADS, XATTN_HEAD_DIM)
    k = (mem @ wk).reshape(bsz, -1, XATTN_HEADS, XATTN_HEAD_DIM)
    v = (mem @ wv).reshape(bsz, -1, XATTN_HEADS, XATTN_HEAD_DIM)
    s = jnp.einsum("blhd,bmhd->bhlm", q, k).astype(jnp.float32) * (XATTN_HEAD_DIM ** -0.5)
    p = jax.nn.softmax(s, axis=-1).astype(v.dtype)
    o = jnp.einsum("bhlm,bmhd->blhd", p, v).reshape(bsz, L, D_MODEL)
    return o @ wo


def _swiglu(hn, w_gate, w_up, w_down):
    return (jax.nn.silu(hn @ w_gate) * (hn @ w_up)) @ w_down


def setup_inputs(seed: int = 0) -> dict:
    key = jax.random.key(seed)
    ks = jax.random.split(key, 24)
    nrm = lambda k, shp, fan: jax.random.normal(k, shp, jnp.float32) * (fan ** -0.5)
    gain = lambda k, shp: 1.0 + 0.05 * jax.random.normal(k, shp, jnp.float32)
    x = jax.random.normal(ks[0], (BATCH, SEQ, D_MODEL), jnp.float32)
    mem = jax.random.normal(ks[1], (BATCH, N_MEM, D_MODEL), jnp.float32)
    offset = jax.random.randint(ks[2], (BATCH, 1), 0, POS_OFFSET_MAX, dtype=jnp.int32)
    positions = offset + jnp.arange(SEQ, dtype=jnp.int32)[None, :]
    dt0 = jnp.exp(jax.random.uniform(ks[6], (DEPTH, SSD_HEADS), jnp.float32,
                                     math.log(1e-3), math.log(1e-1)))
    dt_bias = dt0 + jnp.log(-jnp.expm1(-dt0))
    a_log = jnp.log(jax.random.uniform(ks[7], (DEPTH, SSD_HEADS), jnp.float32, 1.0, 16.0))
    return {
        "x": x,
        "mem": mem,
        "positions": positions,
        "norm_mix_g": gain(ks[3], (DEPTH, D_MODEL)),
        "w_in": nrm(ks[4], (DEPTH, D_MODEL, IN_COLS), D_MODEL),
        "conv_w": nrm(ks[5], (DEPTH, CONV_WIDTH, CONV_DIM), CONV_WIDTH),
        "conv_b": 0.02 * jax.random.normal(ks[8], (DEPTH, CONV_DIM), jnp.float32),
        "dt_bias": dt_bias,
        "a_log": a_log,
        "d_skip": gain(ks[9], (DEPTH, SSD_HEADS)),
        "ssd_norm_g": gain(ks[10], (DEPTH, SSD_WIDTH)),
        "w_out": nrm(ks[11], (DEPTH, D_MIX, D_MODEL), D_MIX),
        "norm_xattn_g": gain(ks[12], (DEPTH, D_MODEL)),
        "w_xq": nrm(ks[13], (DEPTH, D_MODEL, D_MODEL), D_MODEL),
        "w_xk": nrm(ks[14], (DEPTH, D_MODEL, D_MODEL), D_MODEL),
        "w_xv": nrm(ks[15], (DEPTH, D_MODEL, D_MODEL), D_MODEL),
        "w_xo": nrm(ks[16], (DEPTH, D_MODEL, D_MODEL), D_MODEL),
        "norm_ffn_g": gain(ks[17], (DEPTH, D_MODEL)),
        "w_gate": nrm(ks[18], (DEPTH, D_MODEL, D_FF), D_MODEL),
        "w_up": nrm(ks[19], (DEPTH, D_MODEL, D_FF), D_MODEL),
        "w_down": nrm(ks[20], (DEPTH, D_FF, D_MODEL), D_FF),
        "norm_final_g": gain(ks[21], (D_MODEL,)),
    }


def reference(x, mem, positions, norm_mix_g, w_in, conv_w, conv_b, dt_bias, a_log, d_skip,
              ssd_norm_g, w_out, norm_xattn_g, w_xq, w_xk, w_xv, w_xo, norm_ffn_g,
              w_gate, w_up, w_down, norm_final_g):
    pos = positions.astype(jnp.float32)
    h = x
    for i in range(DEPTH):
        hn = _rmsnorm(h, norm_mix_g[i])
        h = h + _hybrid_mixer(hn, pos, w_in[i], conv_w[i], conv_b[i], dt_bias[i], a_log[i],
                              d_skip[i], ssd_norm_g[i], w_out[i])
        hn = _rmsnorm(h, norm_xattn_g[i])
        h = h + _cross_attn(hn, mem, w_xq[i], w_xk[i], w_xv[i], w_xo[i])
        hn = _rmsnorm(h, norm_ffn_g[i])
        h = h + _swiglu(hn, w_gate[i], w_up[i], w_down[i])
    return _rmsnorm(h, norm_final_g)
```

```python
import functools

import jax
import jax.numpy as jnp
from jax import lax
from jax.experimental import pallas as pl
from jax.experimental.pallas import tpu as pltpu

F32 = jnp.float32
BF16 = jnp.bfloat16

SSD_HEAD_DIM = 64
SSD_GROUPS = 2
SSD_STATE = 128
RET_HEADS = 4
XATTN_HEADS = 4
CHUNK = 128
ROPE_BASE = 10000.0
EPS = 1e-6

V7X_VMEM_BYTES = 64 * 1024 * 1024
VMEM_LIMIT_BYTES = V7X_VMEM_BYTES - 6 * 1024 * 1024
LANES = 128
SUBLANES = 8

_NT = (((1,), (1,)), ((), ()))
_TN = (((0,), (0,)), ((), ()))


def _mm(a, b):
    return jnp.dot(a, b, preferred_element_type=F32)


def _mm_nt(a, b):
    return lax.dot_general(a, b, _NT, preferred_element_type=F32)


def _mm_tn(a, b):
    return lax.dot_general(a, b, _TN, preferred_element_type=F32)


def _split_bf16(x, parts):
    out = []
    r = x
    for _ in range(parts):
        p = r.astype(BF16)
        out.append(p)
        r = r - p.astype(F32)
    return out


def _mm_split_lhs(x, m, parts):
    acc = None
    for p in _split_bf16(x, parts):
        t = _mm(p, m)
        acc = t if acc is None else acc + t
    return acc


def _mm_split_rhs(m, x, parts):
    acc = None
    for p in _split_bf16(x, parts):
        t = _mm(m, p)
        acc = t if acc is None else acc + t
    return acc


def _silu(x):
    return x / (1.0 + jnp.exp(-x))


def _softplus(x):
    return jnp.maximum(x, 0.0) + jnp.log1p(jnp.exp(-jnp.abs(x)))


def _rms(x):
    return x * lax.rsqrt(jnp.mean(x * x, axis=-1, keepdims=True) + EPS)


def _mixer_kernel(
    x_ref, pos_ref, invf_ref, gmix_ref,
    wz_ref, wxbc_ref, wdt_ref, wdtt_ref, wq_ref, wk_ref, wv_ref, wg_ref,
    convw_ref, convb_ref, dtb_row_ref, dtb_col_ref, alog_row_ref, alog_col_ref,
    dexp_ref, gssd_ref, expand_ref, ltri_ref, utri_ref,
    intra_ref, xi_ref, zeta_ref, cdec_ref, wout_ref,
    o_ref,
    cbuf, xs_s, b_s, c_s, sz_s, sg_s, q_s, k_s, v_s, dtc_s, dtt_s, y_s, st_s, rs_s,
    *, tl, ssd_w, n_heads, conv_k, ret_w):
    j = pl.program_id(1)
    n_chunks = tl // CHUNK
    hpg = n_heads // SSD_GROUPS
    gw = hpg * SSD_HEAD_DIM
    rd = ret_w // RET_HEADS
    half = rd // 2
    gn = SSD_GROUPS * SSD_STATE
    pad = SUBLANES

    @pl.when(j == 0)
    def _():
        cbuf[0:pad, :] = jnp.zeros((pad, cbuf.shape[1]), F32)
        st_s[...] = jnp.zeros_like(st_s)
        rs_s[...] = jnp.zeros_like(rs_s)

    x = x_ref[0]
    hn = (_rms(x) * gmix_ref[...]).astype(BF16)

    sz_s[...] = _silu(_mm(hn, wz_ref[...])).astype(sz_s.dtype)

    cbuf[pad:pad + tl, :] = _mm(hn, wxbc_ref[...])
    acc = convb_ref[...]
    for kk in range(conv_k):
        off = pad - (conv_k - 1) + kk
        acc = acc + convw_ref[kk:kk + 1, :] * cbuf[off:off + tl, :]
    cbuf[0:pad, :] = cbuf[tl:tl + pad, :]
    act = _silu(acc)
    xs_s[...] = act[:, :ssd_w]
    b_s[...] = act[:, ssd_w:ssd_w + gn].astype(BF16)
    c_s[...] = act[:, ssd_w + gn:ssd_w + 2 * gn].astype(BF16)

    dtc_s[...] = _softplus(_mm(hn, wdt_ref[...]) + dtb_row_ref[...])
    dtt = _softplus(_mm_nt(wdtt_ref[...], hn) + dtb_col_ref[...])
    for c in range(n_chunks):
        dtt_s[c] = dtt[:, c * CHUNK:(c + 1) * CHUNK]

    ang = pos_ref[0] * invf_ref[...]
    cs = jnp.cos(ang)
    sn = jnp.sin(ang)
    kscale = float(rd) ** -0.5
    q = _mm(hn, wq_ref[...])
    for h in range(RET_HEADS):
        a0 = q[:, h * rd:h * rd + half]
        a1 = q[:, h * rd + half:(h + 1) * rd]
        q_s[:, h * rd:h * rd + half] = (a0 * cs - a1 * sn).astype(BF16)
        q_s[:, h * rd + half:(h + 1) * rd] = (a1 * cs + a0 * sn).astype(BF16)
    k = _mm(hn, wk_ref[...])
    for h in range(RET_HEADS):
        a0 = k[:, h * rd:h * rd + half]
        a1 = k[:, h * rd + half:(h + 1) * rd]
        k_s[:, h * rd:h * rd + half] = ((a0 * cs - a1 * sn) * kscale).astype(BF16)
        k_s[:, h * rd + half:(h + 1) * rd] = ((a1 * cs + a0 * sn) * kscale).astype(BF16)
    v_s[...] = _mm(hn, wv_ref[...]).astype(BF16)
    sg_s[...] = _silu(_mm(hn, wg_ref[...])).astype(sg_s.dtype)

    a_row = -jnp.exp(alog_row_ref[...])
    a_col = -jnp.exp(alog_col_ref[...])
    row_id = lax.broadcasted_iota(jnp.int32, (CHUNK, CHUNK), 0)
    col_id = lax.broadcasted_iota(jnp.int32, (CHUNK, CHUNK), 1)
    causal = row_id >= col_id
    low_half = col_id < SSD_HEAD_DIM

    def chunk_body(c, carry):
        r0 = pl.multiple_of(c * CHUNK, CHUNK)
        rows = pl.ds(r0, CHUNK)

        dt_c = dtc_s[rows, :]
        dtt_c = dtt_s[c]
        acs = _mm_split_rhs(ltri_ref[...], dt_c * a_row, 3)
        acst = _mm_split_lhs(dtt_c * a_col, utri_ref[...], 3)
        expand = expand_ref[...]
        acs_e = _mm_split_lhs(acs, expand, 3)
        dt_e = _mm_split_lhs(dt_c, expand, 2)
        last_e = acs_e[CHUNK - 1:CHUNK, :]
        eacs_e = jnp.exp(acs_e)
        elast_e = jnp.exp(last_e)
        xs_c = xs_s[rows, :]
        xdt = xs_c * dt_e
        xdt_b = xdt.astype(BF16)
        xw_b = (xdt * jnp.exp(last_e - acs_e)).astype(BF16)

        ys = []
        for g in range(SSD_GROUPS):
            bg = b_s[rows, g * SSD_STATE:(g + 1) * SSD_STATE]
            cg = c_s[rows, g * SSD_STATE:(g + 1) * SSD_STATE]
            cb = _mm_nt(cg, bg)
            st = st_s[g]
            yoff = _mm(cg, st.astype(BF16)) * eacs_e[:, g * gw:(g + 1) * gw]
            for pr in range(hpg // 2):
                h0 = g * hpg + 2 * pr
                ms = []
                for hh in (h0, h0 + 1):
                    seg = (jnp.broadcast_to(acs[:, hh:hh + 1], (CHUNK, CHUNK))
                           - jnp.broadcast_to(acst[hh:hh + 1, :], (CHUNK, CHUNK)))
                    dec = jnp.exp(jnp.where(causal, seg, -jnp.inf))
                    ms.append((cb * dec).astype(BF16))
                mcat = jnp.concatenate(ms, axis=1)
                xp = xdt_b[:, h0 * SSD_HEAD_DIM:(h0 + 2) * SSD_HEAD_DIM]
                zero = jnp.zeros_like(xp)
                xbd = jnp.concatenate([jnp.where(low_half, xp, zero),
                                       jnp.where(low_half, zero, xp)], axis=0)
                ys.append(_mm(mcat, xbd) + yoff[:, 2 * pr * SSD_HEAD_DIM:(2 * pr + 2) * SSD_HEAD_DIM])
            st_s[g] = (st * elast_e[:, g * gw:(g + 1) * gw]
                       + _mm_tn(bg, xw_b[:, g * gw:(g + 1) * gw]))
        y = jnp.concatenate(ys, axis=1)
        y = (y + dexp_ref[...] * xs_c) * sz_s[rows, :].astype(F32)
        y_s[rows, 0:ssd_w] = (_rms(y) * gssd_ref[...]).astype(BF16)

        for h in range(RET_HEADS):
            qh = q_s[rows, h * rd:(h + 1) * rd]
            kh = k_s[rows, h * rd:(h + 1) * rd]
            vh = v_s[rows, h * rd:(h + 1) * rd]
            sc = _mm_nt(qh, kh) * intra_ref[h]
            s_h = rs_s[h]
            xi_h = jnp.broadcast_to(xi_ref[:, h:h + 1], (CHUNK, rd))
            o = _mm(sc.astype(BF16), vh) + _mm(qh, s_h.astype(BF16)) * xi_h
            kz = (kh.astype(F32) * jnp.broadcast_to(zeta_ref[:, h:h + 1], (CHUNK, rd))).astype(BF16)
            cd_h = jnp.broadcast_to(cdec_ref[0:1, h:h + 1], (rd, rd))
            rs_s[h] = s_h * cd_h + _mm_tn(kz, vh)
            y_s[rows, ssd_w + h * rd:ssd_w + (h + 1) * rd] = (
                sg_s[rows, h * rd:(h + 1) * rd].astype(F32) * _rms(o)).astype(BF16)
        return carry

    lax.fori_loop(0, n_chunks, chunk_body, 0)

    o_ref[0] = x_ref[0] + _mm(y_s[...], wout_ref[...])


def _const_spec(shape):
    zeros = (0,) * len(shape)
    return pl.BlockSpec(shape, lambda b, j: zeros, pipeline_mode=pl.Buffered(1))


def _mixer_tile(seq):
    for t in (256, 128):
        if seq % t == 0:
            return t
    raise ValueError(f"sequence length {seq} is not a multiple of the chunk length {CHUNK}")


def _mixer_call(x, pos, p):
    bsz, seq, d = x.shape
    tl = _mixer_tile(seq)
    ssd_w = p["wz"].shape[1]
    n_heads = p["wdt"].shape[1]
    conv_dim = p["wxbc"].shape[1]
    conv_k = p["conv_w"].shape[0]
    ret_w = p["wq"].shape[1]
    d_mix = p["wout"].shape[0]
    hpg = n_heads // SSD_GROUPS
    rd = ret_w // RET_HEADS
    assert conv_dim == ssd_w + 2 * SSD_GROUPS * SSD_STATE and d_mix == ssd_w + ret_w
    assert ssd_w == n_heads * SSD_HEAD_DIM and hpg % 2 == 0 and (rd // 2) % LANES == 0

    consts = [p["inv_freq"], p["gmix"], p["wz"], p["wxbc"], p["wdt"], p["wdtt"], p["wq"], p["wk"],
              p["wv"], p["wg"], p["conv_w"], p["conv_b"], p["dtb_row"], p["dtb_col"], p["alog_row"],
              p["alog_col"], p["dexp"], p["gssd"], p["expand"], p["ltri"], p["utri"], p["intra"],
              p["xi"], p["zeta"], p["cdec"], p["wout"]]
    in_specs = [
        pl.BlockSpec((1, tl, d), lambda b, j: (b, j, 0)),
        pl.BlockSpec((1, tl, 1), lambda b, j: (b, j, 0)),
    ] + [_const_spec(c.shape) for c in consts]
    kern = functools.partial(_mixer_kernel, tl=tl, ssd_w=ssd_w, n_heads=n_heads, conv_k=conv_k, ret_w=ret_w)
    return pl.pallas_call(
        kern,
        grid=(bsz, seq // tl),
        in_specs=in_specs,
        out_specs=pl.BlockSpec((1, tl, d), lambda b, j: (b, j, 0)),
        out_shape=jax.ShapeDtypeStruct((bsz, seq, d), F32),
        scratch_shapes=[
            pltpu.VMEM((tl + SUBLANES, conv_dim), F32),
            pltpu.VMEM((tl, ssd_w), F32),
            pltpu.VMEM((tl, SSD_GROUPS * SSD_STATE), BF16),
            pltpu.VMEM((tl, SSD_GROUPS * SSD_STATE), BF16),
            pltpu.VMEM((tl, ssd_w), F32),
            pltpu.VMEM((tl, ret_w), F32),
            pltpu.VMEM((tl, ret_w), BF16),
            pltpu.VMEM((tl, ret_w), BF16),
            pltpu.VMEM((tl, ret_w), BF16),
            pltpu.VMEM((tl, n_heads), F32),
            pltpu.VMEM((tl // CHUNK, n_heads, CHUNK), F32),
            pltpu.VMEM((tl, d_mix), BF16),
            pltpu.VMEM((SSD_GROUPS, SSD_STATE, hpg * SSD_HEAD_DIM), F32),
            pltpu.VMEM((RET_HEADS, rd, rd), F32),
        ],
        compiler_params=pltpu.CompilerParams(
            dimension_semantics=("arbitrary", "arbitrary"), vmem_limit_bytes=VMEM_LIMIT_BYTES),
        name="mixer",
    )(x, pos, *consts)


def _kv_kernel(mem_ref, wk_ref, wv_ref, k_ref, v_ref):
    m = mem_ref[0].astype(BF16)
    k_ref[0] = _mm(m, wk_ref[...]).astype(BF16)
    v_ref[0] = _mm(m, wv_ref[...]).astype(BF16)


def _kv_call(mem, wk, wv):
    bsz, n_mem, d = mem.shape
    const = lambda s: pl.BlockSpec(s, lambda b: (0,) * len(s), pipeline_mode=pl.Buffered(1))
    return pl.pallas_call(
        _kv_kernel,
        grid=(bsz,),
        in_specs=[pl.BlockSpec((1, n_mem, d), lambda b: (b, 0, 0)), const(wk.shape), const(wv.shape)],
        out_specs=[pl.BlockSpec((1, n_mem, wk.shape[1]), lambda b: (b, 0, 0)),
                   pl.BlockSpec((1, n_mem, wv.shape[1]), lambda b: (b, 0, 0))],
        out_shape=[jax.ShapeDtypeStruct((bsz, n_mem, wk.shape[1]), BF16),
                   jax.ShapeDtypeStruct((bsz, n_mem, wv.shape[1]), BF16)],
        compiler_params=pltpu.CompilerParams(
            dimension_semantics=("arbitrary",), vmem_limit_bytes=VMEM_LIMIT_BYTES),
        name="xattn_kv",
    )(mem, wk, wv)


def _xattn_ffn_kernel(h_ref, k_ref, v_ref, gx_ref, wq_ref, wo_ref, gf_ref, wgate_ref, wup_ref,
                      wdown_ref, gfin_ref, o_ref, att_s, *, final_norm):
    h = h_ref[0]
    d = h.shape[1]
    hd = d // XATTN_HEADS
    hn = (_rms(h) * gx_ref[...]).astype(BF16)
    q = _mm(hn, wq_ref[...]).astype(BF16)
    scale = float(hd) ** -0.5
    for a in range(XATTN_HEADS):
        cols = slice(a * hd, (a + 1) * hd)
        s = _mm_nt(q[:, cols], k_ref[0, :, cols]) * scale
        e = jnp.exp(s - jnp.max(s, axis=-1, keepdims=True))
        pr = e / jnp.sum(e, axis=-1, keepdims=True)
        att_s[:, cols] = _mm(pr.astype(BF16), v_ref[0, :, cols]).astype(BF16)
    h = h + _mm(att_s[...], wo_ref[...])

    hn = (_rms(h) * gf_ref[...]).astype(BF16)
    act = (_silu(_mm(hn, wgate_ref[...])) * _mm(hn, wup_ref[...])).astype(BF16)
    h = h + _mm(act, wdown_ref[...])
    if final_norm:
        h = _rms(h) * gfin_ref[...]
    o_ref[0] = h


def _xattn_tile(seq):
    for t in (512, 256, 128):
        if seq % t == 0:
            return t
    raise ValueError(f"sequence length {seq} is not a multiple of {CHUNK}")


def _xattn_ffn_call(h, k, v, p, final_norm):
    bsz, seq, d = h.shape
    tl = _xattn_tile(seq)
    n_mem = k.shape[1]
    consts = [p["gx"], p["wxq"], p["wxo"], p["gf"], p["wgate"], p["wup"], p["wdown"], p["gfin"]]
    kern = functools.partial(_xattn_ffn_kernel, final_norm=final_norm)
    return pl.pallas_call(
        kern,
        grid=(bsz, seq // tl),
        in_specs=[
            pl.BlockSpec((1, tl, d), lambda b, j: (b, j, 0)),
            pl.BlockSpec((1, n_mem, d), lambda b, j: (b, 0, 0)),
            pl.BlockSpec((1, n_mem, d), lambda b, j: (b, 0, 0)),
        ] + [_const_spec(c.shape) for c in consts],
        out_specs=pl.BlockSpec((1, tl, d), lambda b, j: (b, j, 0)),
        out_shape=jax.ShapeDtypeStruct((bsz, seq, d), F32),
        scratch_shapes=[pltpu.VMEM((tl, d), BF16)],
        compiler_params=pltpu.CompilerParams(
            dimension_semantics=("arbitrary", "arbitrary"), vmem_limit_bytes=VMEM_LIMIT_BYTES),
        name="xattn_ffn",
    )(h, k, v, *consts)


def _retention_constants():
    idx = jnp.arange(CHUNK, dtype=F32)
    log_gamma = jnp.log(1.0 - 2.0 ** (-5.0 - jnp.arange(RET_HEADS, dtype=F32)))
    rel = idx[:, None] - idx[None, :]
    intra = jnp.where(rel[None] >= 0, jnp.exp(jnp.maximum(rel, 0.0)[None] * log_gamma[:, None, None]), 0.0)
    xi = jnp.exp((idx + 1.0)[None] * log_gamma[:, None])
    zeta = jnp.exp((CHUNK - 1.0 - idx)[None] * log_gamma[:, None])
    cdec = jnp.exp(CHUNK * log_gamma)
    return intra, xi.T, zeta.T, cdec[None, :]


def _mixer_params(norm_g, w_in, conv_w, conv_b, dt_bias, a_log, d_skip, ssd_norm_g, w_out):
    d, in_cols = w_in.shape
    n_heads = dt_bias.shape[0]
    ssd_w = n_heads * SSD_HEAD_DIM
    conv_dim = conv_w.shape[1]
    ret_w = (in_cols - ssd_w - conv_dim - n_heads) // 4
    rd = ret_w // RET_HEADS
    o0 = ssd_w
    o1 = o0 + conv_dim
    o2 = o1 + n_heads
    perm = (jnp.arange(RET_HEADS)[:, None] * rd
            + jnp.concatenate([jnp.arange(0, rd, 2), jnp.arange(1, rd, 2)])[None, :]).reshape(-1)
    wq = w_in[:, o2:o2 + ret_w][:, perm]
    wk = w_in[:, o2 + ret_w:o2 + 2 * ret_w][:, perm]
    wdt = w_in[:, o1:o2]
    intra, xi, zeta, cdec = _retention_constants()
    tri = jnp.tril(jnp.ones((CHUNK, CHUNK), F32))
    return {
        "inv_freq": (1.0 / (ROPE_BASE ** jnp.linspace(0.0, 1.0, rd // 2, dtype=F32)))[None, :],
        "gmix": norm_g[None, :].astype(F32),
        "wz": w_in[:, :o0].astype(BF16),
        "wxbc": w_in[:, o0:o1].astype(BF16),
        "wdt": wdt.astype(BF16),
        "wdtt": wdt.T.astype(BF16),
        "wq": wq.astype(BF16),
        "wk": wk.astype(BF16),
        "wv": w_in[:, o2 + 2 * ret_w:o2 + 3 * ret_w].astype(BF16),
        "wg": w_in[:, o2 + 3 * ret_w:].astype(BF16),
        "conv_w": conv_w.astype(F32),
        "conv_b": conv_b[None, :].astype(F32),
        "dtb_row": dt_bias[None, :].astype(F32),
        "dtb_col": dt_bias[:, None].astype(F32),
        "alog_row": a_log[None, :].astype(F32),
        "alog_col": a_log[:, None].astype(F32),
        "dexp": jnp.repeat(d_skip.astype(F32), SSD_HEAD_DIM)[None, :],
        "gssd": ssd_norm_g[None, :].astype(F32),
        "expand": jnp.repeat(jnp.eye(n_heads, dtype=F32), SSD_HEAD_DIM, axis=1).astype(BF16),
        "ltri": tri.astype(BF16),
        "utri": tri.T.astype(BF16),
        "intra": intra, "xi": xi, "zeta": zeta, "cdec": cdec,
        "wout": w_out.astype(BF16),
    }


def kernel(x, mem, positions, norm_mix_g, w_in, conv_w, conv_b, dt_bias, a_log, d_skip, ssd_norm_g, w_out, norm_xattn_g, w_xq, w_xk, w_xv, w_xo, norm_ffn_g, w_gate, w_up, w_down, norm_final_g):
    depth = w_in.shape[0]
    pos = positions.astype(F32)[:, :, None]
    h = x
    for i in range(depth):
        mp = _mixer_params(norm_mix_g[i], w_in[i], conv_w[i], conv_b[i], dt_bias[i], a_log[i],
                           d_skip[i], ssd_norm_g[i], w_out[i])
        h = _mixer_call(h, pos, mp)
        k, v = _kv_call(mem, w_xk[i].astype(BF16), w_xv[i].astype(BF16))
        xp = {
            "gx": norm_xattn_g[i][None, :].astype(F32), "wxq": w_xq[i].astype(BF16),
            "wxo": w_xo[i].astype(BF16), "gf": norm_ffn_g[i][None, :].astype(F32),
            "wgate": w_gate[i].astype(BF16), "wup": w_up[i].astype(BF16),
            "wdown": w_down[i].astype(BF16), "gfin": norm_final_g[None, :].astype(F32),
        }
        h = _xattn_ffn_call(h, k, v, xp, final_norm=(i == depth - 1))
    return h
```

```python
import functools

import jax
import jax.numpy as jnp
from jax import lax
from jax.experimental import pallas as pl
from jax.experimental.pallas import tpu as pltpu

F32 = jnp.float32
BF16 = jnp.bfloat16

SSD_HEAD_DIM = 64
SSD_GROUPS = 2
SSD_STATE = 128
SSD_CHUNK = 128
RET_HEADS = 4
RET_CHUNK = 256
XATTN_HEADS = 4
ROPE_BASE = 10000.0
EPS = 1e-6

V7X_VMEM_BYTES = 64 * 1024 * 1024
VMEM_LIMIT_BYTES = V7X_VMEM_BYTES - 6 * 1024 * 1024
LANES = 128
SUBLANES = 8

_NT = (((1,), (1,)), ((), ()))
_TN = (((0,), (0,)), ((), ()))


def _mm(a, b):
    return jnp.dot(a, b, preferred_element_type=F32)


def _mm_nt(a, b):
    return lax.dot_general(a, b, _NT, preferred_element_type=F32)


def _mm_tn(a, b):
    return lax.dot_general(a, b, _TN, preferred_element_type=F32)


def _pack_rows(w):
    k, n = w.shape
    pairs = jnp.swapaxes(w.astype(BF16).reshape(k // 2, 2, n), 1, 2)
    return lax.bitcast_convert_type(pairs, jnp.uint32)


def _unpack(w_packed):
    return pltpu.bitcast(w_packed, BF16)


def _split_bf16(x, parts):
    out = []
    r = x
    for _ in range(parts):
        p = r.astype(BF16)
        out.append(p)
        r = r - p.astype(F32)
    return out


def _mm_split_lhs(x, m, parts):
    acc = None
    for p in _split_bf16(x, parts):
        t = _mm(p, m)
        acc = t if acc is None else acc + t
    return acc


def _mm_split_rhs(m, x, parts):
    acc = None
    for p in _split_bf16(x, parts):
        t = _mm(m, p)
        acc = t if acc is None else acc + t
    return acc


def _silu(x):
    return (0.5 * x) * (1.0 + jnp.tanh(0.5 * x))


def _softplus(x):
    return jnp.maximum(x, 0.0) + jnp.log1p(jnp.exp(-jnp.abs(x)))


def _rms(x):
    return x * lax.rsqrt(jnp.mean(x * x, axis=-1, keepdims=True) + EPS)


def _mixer_kernel(
    x_ref, pos_ref, invf_ref, gmix_ref,
    wz_ref, wxbc_ref, wdt_ref, wdtt_ref, wq_ref, wk_ref, wv_ref, wg_ref,
    convw_ref, convb_ref, dtb_row_ref, dtb_col_ref, alog_row_ref, alog_col_ref,
    dexp_ref, gssd_ref, expand_ref, ltri_ref, utri_ref,
    intra_ref, xi_ref, zeta_ref, cdec_ref, wout_ref,
    o_ref,
    cbuf, xs_s, b_s, c_s, e1_s, xw_s, yraw_s, y_s, st_s, rs_s,
    *, tl, ssd_w, n_heads, conv_k, ret_w):
    j = pl.program_id(1)
    n_chunks = tl // SSD_CHUNK
    hpg = n_heads // SSD_GROUPS
    gw = hpg * SSD_HEAD_DIM
    rd = ret_w // RET_HEADS
    half = rd // 2
    gn = SSD_GROUPS * SSD_STATE
    pad = SUBLANES
    ck = SSD_CHUNK

    @pl.when(j == 0)
    def _():
        cbuf[0:pad, :] = jnp.zeros((pad, cbuf.shape[1]), F32)
        st_s[...] = jnp.zeros_like(st_s)
        rs_s[...] = jnp.zeros_like(rs_s)

    x = x_ref[0]
    hn = (_rms(x) * gmix_ref[...]).astype(BF16)

    def conv_proj():
        cbuf[pad:pad + tl, :] = _mm(hn, _unpack(wxbc_ref[...]))

    def conv_stage():
        acc = convb_ref[...]
        for kk in range(conv_k):
            off = pad - (conv_k - 1) + kk
            acc = acc + convw_ref[kk:kk + 1, :] * cbuf[off:off + tl, :]
        cbuf[0:pad, :] = cbuf[tl:tl + pad, :]
        act = _silu(acc)
        xs_s[...] = act[:, :ssd_w]
        b_s[...] = act[:, ssd_w:ssd_w + gn].astype(BF16)
        c_s[...] = act[:, ssd_w + gn:ssd_w + 2 * gn].astype(BF16)

    tile_decay = {}

    def decay_stage():
        dt = _softplus(_mm(hn, wdt_ref[...]) + dtb_row_ref[...])
        dtt = _softplus(_mm_nt(wdtt_ref[...], hn) + dtb_col_ref[...])
        acs = _mm_split_rhs(ltri_ref[...], dt * (-jnp.exp(alog_row_ref[...])), 3)
        acst = _mm_split_lhs(dtt * (-jnp.exp(alog_col_ref[...])), utri_ref[...], 3)
        last = jnp.concatenate(
            [jnp.broadcast_to(acs[(c + 1) * ck - 1:(c + 1) * ck, :], (ck, n_heads)) for c in range(n_chunks)],
            axis=0)
        expand = expand_ref[...]
        e1_s[...] = _mm_split_lhs(jnp.exp(acs), expand, 2)
        xw_s[...] = (xs_s[...] * _mm((dt * jnp.exp(last - acs)).astype(BF16), expand)).astype(BF16)
        tile_decay.update(dtt=dtt, acs=acs, acst=acst)

    row_id = lax.broadcasted_iota(jnp.int32, (ck, ck), 0)
    col_id = lax.broadcasted_iota(jnp.int32, (ck, ck), 1)
    causal = row_id >= col_id
    low_half = col_id < SSD_HEAD_DIM

    def ssd_chunk(c):
        acs, acst, dtt = tile_decay["acs"], tile_decay["acst"], tile_decay["dtt"]
        rows = slice(c * ck, (c + 1) * ck)
        for g in range(SSD_GROUPS):
            gcols = slice(g * gw, (g + 1) * gw)
            bg = b_s[rows, g * SSD_STATE:(g + 1) * SSD_STATE]
            cg = c_s[rows, g * SSD_STATE:(g + 1) * SSD_STATE]
            cb = _mm_nt(cg, bg)
            st = st_s[g]
            yoff = _mm(cg, st.astype(BF16)) * e1_s[rows, gcols]
            for pr in range(hpg // 2):
                h0 = g * hpg + 2 * pr
                ms = []
                for hh in (h0, h0 + 1):
                    seg = (jnp.broadcast_to(acs[rows, hh:hh + 1], (ck, ck))
                           - jnp.broadcast_to(acst[hh:hh + 1, rows], (ck, ck)))
                    dec = jnp.exp(jnp.where(causal, seg, -jnp.inf))
                    ms.append((cb * dec * jnp.broadcast_to(dtt[hh:hh + 1, rows], (ck, ck))).astype(BF16))
                mcat = jnp.concatenate(ms, axis=1)
                pcols = slice(h0 * SSD_HEAD_DIM, (h0 + 2) * SSD_HEAD_DIM)
                xp = xs_s[rows, pcols].astype(BF16)
                zero = jnp.zeros_like(xp)
                xbd = jnp.concatenate([jnp.where(low_half, xp, zero),
                                       jnp.where(low_half, zero, xp)], axis=0)
                yraw_s[rows, pcols] = (
                    _mm(mcat, xbd) + yoff[:, 2 * pr * SSD_HEAD_DIM:(2 * pr + 2) * SSD_HEAD_DIM])
            elast = e1_s[(c + 1) * ck - 1:(c + 1) * ck, gcols]
            st_s[g] = st * elast + _mm_tn(bg, xw_s[rows, gcols])

    def ssd_finish():
        y = (yraw_s[...] + dexp_ref[...] * xs_s[...]) * _silu(_mm(hn, _unpack(wz_ref[...])))
        y_s[:, 0:ssd_w] = (_rms(y) * gssd_ref[...]).astype(BF16)

    ang = pos_ref[0] * invf_ref[...]
    cs = jnp.cos(ang)
    sn = jnp.sin(ang)
    kscale = float(rd) ** -0.5
    rc = RET_CHUNK

    def ret_head(h):
        hc = slice(h * rd, (h + 1) * rd)
        qp = _mm(hn, _unpack(wq_ref[:, hc]))
        kp = _mm(hn, _unpack(wk_ref[:, hc]))
        q0, q1 = qp[:, :half], qp[:, half:]
        k0, k1 = kp[:, :half], kp[:, half:]
        qh = jnp.concatenate([q0 * cs - q1 * sn, q1 * cs + q0 * sn], axis=1).astype(BF16)
        kr = jnp.concatenate([k0 * cs - k1 * sn, k1 * cs + k0 * sn], axis=1) * kscale
        kh = kr.astype(BF16)
        vh = _mm(hn, _unpack(wv_ref[:, hc])).astype(BF16)
        gate = _silu(_mm(hn, _unpack(wg_ref[:, hc])))
        for r in range(tl // rc):
            rr = slice(r * rc, (r + 1) * rc)
            sc = _mm_nt(qh[rr], kh[rr]) * intra_ref[h]
            s_h = rs_s[h]
            xi_h = jnp.broadcast_to(xi_ref[:, h:h + 1], (rc, rd))
            o = _mm(sc.astype(BF16), vh[rr]) + _mm(qh[rr], s_h.astype(BF16)) * xi_h
            kz = (kr[rr] * jnp.broadcast_to(zeta_ref[:, h:h + 1], (rc, rd))).astype(BF16)
            cd_h = jnp.broadcast_to(cdec_ref[0:1, h:h + 1], (rd, rd))
            rs_s[h] = s_h * cd_h + _mm_tn(kz, vh[rr])
            y_s[rr, ssd_w + h * rd:ssd_w + (h + 1) * rd] = (gate[rr] * _rms(o)).astype(BF16)

    chunks = [functools.partial(ssd_chunk, c) for c in range(n_chunks)]
    heads = [functools.partial(ret_head, h) for h in range(RET_HEADS)]
    conv_proj()
    conv_stage()
    decay_stage()
    for stage in chunks + [ssd_finish]:
        if heads:
            heads.pop(0)()
        stage()
    for head in heads:
        head()

    o_ref[0] = x_ref[0] + _mm(y_s[...], _unpack(wout_ref[...]))


def _const_spec(shape):
    zeros = (0,) * len(shape)
    return pl.BlockSpec(shape, lambda b, j: zeros, pipeline_mode=pl.Buffered(1))


def _mixer_tile(seq):
    if seq % RET_CHUNK:
        raise ValueError(f"sequence length {seq} is not a multiple of {RET_CHUNK}")
    return RET_CHUNK


def _mixer_call(x, pos, p):
    bsz, seq, d = x.shape
    tl = _mixer_tile(seq)
    ssd_w = p["wz"].shape[1]
    n_heads = p["wdt"].shape[1]
    conv_dim = p["wxbc"].shape[1]
    conv_k = p["conv_w"].shape[0]
    ret_w = p["wq"].shape[1]
    d_mix = 2 * p["wout"].shape[0]
    hpg = n_heads // SSD_GROUPS
    rd = ret_w // RET_HEADS
    assert conv_dim == ssd_w + 2 * SSD_GROUPS * SSD_STATE and d_mix == ssd_w + ret_w
    assert ssd_w == n_heads * SSD_HEAD_DIM and hpg % 2 == 0 and (rd // 2) % LANES == 0

    consts = [p["inv_freq"], p["gmix"], p["wz"], p["wxbc"], p["wdt"], p["wdtt"], p["wq"], p["wk"],
              p["wv"], p["wg"], p["conv_w"], p["conv_b"], p["dtb_row"], p["dtb_col"], p["alog_row"],
              p["alog_col"], p["dexp"], p["gssd"], p["expand"], p["ltri"], p["utri"], p["intra"],
              p["xi"], p["zeta"], p["cdec"], p["wout"]]
    in_specs = [
        pl.BlockSpec((1, tl, d), lambda b, j: (b, j, 0)),
        pl.BlockSpec((1, tl, 1), lambda b, j: (b, j, 0)),
    ] + [_const_spec(c.shape) for c in consts]
    kern = functools.partial(_mixer_kernel, tl=tl, ssd_w=ssd_w, n_heads=n_heads, conv_k=conv_k, ret_w=ret_w)
    return pl.pallas_call(
        kern,
        grid=(bsz, seq // tl),
        in_specs=in_specs,
        out_specs=pl.BlockSpec((1, tl, d), lambda b, j: (b, j, 0)),
        out_shape=jax.ShapeDtypeStruct((bsz, seq, d), F32),
        scratch_shapes=[
            pltpu.VMEM((tl + SUBLANES, conv_dim), F32),
            pltpu.VMEM((tl, ssd_w), F32),
            pltpu.VMEM((tl, SSD_GROUPS * SSD_STATE), BF16),
            pltpu.VMEM((tl, SSD_GROUPS * SSD_STATE), BF16),
            pltpu.VMEM((tl, ssd_w), F32),
            pltpu.VMEM((tl, ssd_w), BF16),
            pltpu.VMEM((tl, ssd_w), F32),
            pltpu.VMEM((tl, d_mix), BF16),
            pltpu.VMEM((SSD_GROUPS, SSD_STATE, hpg * SSD_HEAD_DIM), F32),
            pltpu.VMEM((RET_HEADS, rd, rd), F32),
        ],
        compiler_params=pltpu.CompilerParams(
            dimension_semantics=("arbitrary", "arbitrary"), vmem_limit_bytes=VMEM_LIMIT_BYTES),
        name="mixer",
    )(x, pos, *consts)


def _kv_kernel(mem_ref, wk_ref, wv_ref, k_ref, v_ref):
    m = mem_ref[0].astype(BF16)
    k_ref[0] = _mm(m, _unpack(wk_ref[...])).astype(BF16)
    v_ref[0] = _mm(m, _unpack(wv_ref[...])).astype(BF16)


def _kv_call(mem, wk, wv):
    bsz, n_mem, d = mem.shape
    const = lambda s: pl.BlockSpec(s, lambda b: (0,) * len(s), pipeline_mode=pl.Buffered(1))
    return pl.pallas_call(
        _kv_kernel,
        grid=(bsz,),
        in_specs=[pl.BlockSpec((1, n_mem, d), lambda b: (b, 0, 0)), const(wk.shape), const(wv.shape)],
        out_specs=[pl.BlockSpec((1, n_mem, wk.shape[1]), lambda b: (b, 0, 0)),
                   pl.BlockSpec((1, n_mem, wv.shape[1]), lambda b: (b, 0, 0))],
        out_shape=[jax.ShapeDtypeStruct((bsz, n_mem, wk.shape[1]), BF16),
                   jax.ShapeDtypeStruct((bsz, n_mem, wv.shape[1]), BF16)],
        compiler_params=pltpu.CompilerParams(
            dimension_semantics=("arbitrary",), vmem_limit_bytes=VMEM_LIMIT_BYTES),
        name="xattn_kv",
    )(mem, wk, wv)


def _xattn_ffn_kernel(h_ref, k_ref, v_ref, gx_ref, wq_ref, wo_ref, gf_ref, wgate_ref, wup_ref,
                      wdown_ref, gfin_ref, o_ref, att_s, *, final_norm):
    h = h_ref[0]
    d = h.shape[1]
    hd = d // XATTN_HEADS
    hn = (_rms(h) * gx_ref[...]).astype(BF16)
    q = _mm(hn, _unpack(wq_ref[...])).astype(BF16)
    scale = float(hd) ** -0.5
    for a in range(XATTN_HEADS):
        cols = slice(a * hd, (a + 1) * hd)
        s = _mm_nt(q[:, cols], k_ref[0, :, cols]) * scale
        e = jnp.exp(s - jnp.max(s, axis=-1, keepdims=True))
        pr = e / jnp.sum(e, axis=-1, keepdims=True)
        att_s[:, cols] = _mm(pr.astype(BF16), v_ref[0, :, cols]).astype(BF16)
    h = h + _mm(att_s[...], _unpack(wo_ref[...]))

    hn = (_rms(h) * gf_ref[...]).astype(BF16)
    act = (_silu(_mm(hn, _unpack(wgate_ref[...]))) * _mm(hn, _unpack(wup_ref[...]))).astype(BF16)
    h = h + _mm(act, _unpack(wdown_ref[...]))
    if final_norm:
        h = _rms(h) * gfin_ref[...]
    o_ref[0] = h


def _xattn_tile(seq):
    for t in (512, 256, 128):
        if seq % t == 0:
            return t
    raise ValueError(f"sequence length {seq} is not a multiple of 128")


def _xattn_ffn_call(h, k, v, p, final_norm):
    bsz, seq, d = h.shape
    tl = _xattn_tile(seq)
    n_mem = k.shape[1]
    consts = [p["gx"], p["wxq"], p["wxo"], p["gf"], p["wgate"], p["wup"], p["wdown"], p["gfin"]]
    kern = functools.partial(_xattn_ffn_kernel, final_norm=final_norm)
    return pl.pallas_call(
        kern,
        grid=(bsz, seq // tl),
        in_specs=[
            pl.BlockSpec((1, tl, d), lambda b, j: (b, j, 0)),
            pl.BlockSpec((1, n_mem, d), lambda b, j: (b, 0, 0)),
            pl.BlockSpec((1, n_mem, d), lambda b, j: (b, 0, 0)),
        ] + [_const_spec(c.shape) for c in consts],
        out_specs=pl.BlockSpec((1, tl, d), lambda b, j: (b, j, 0)),
        out_shape=jax.ShapeDtypeStruct((bsz, seq, d), F32),
        scratch_shapes=[pltpu.VMEM((tl, d), BF16)],
        compiler_params=pltpu.CompilerParams(
            dimension_semantics=("arbitrary", "arbitrary"), vmem_limit_bytes=VMEM_LIMIT_BYTES),
        name="xattn_ffn",
    )(h, k, v, *consts)


def _retention_constants():
    idx = jnp.arange(RET_CHUNK, dtype=F32)
    log_gamma = jnp.log(1.0 - 2.0 ** (-5.0 - jnp.arange(RET_HEADS, dtype=F32)))
    rel = idx[:, None] - idx[None, :]
    intra = jnp.where(rel[None] >= 0, jnp.exp(jnp.maximum(rel, 0.0)[None] * log_gamma[:, None, None]), 0.0)
    xi = jnp.exp((idx + 1.0)[None] * log_gamma[:, None])
    zeta = jnp.exp((RET_CHUNK - 1.0 - idx)[None] * log_gamma[:, None])
    cdec = jnp.exp(RET_CHUNK * log_gamma)
    return intra, xi.T, zeta.T, cdec[None, :]


def _block_tril(tl):
    idx = jnp.arange(tl)
    same = (idx[:, None] // SSD_CHUNK) == (idx[None, :] // SSD_CHUNK)
    return jnp.where(same & (idx[:, None] >= idx[None, :]), 1.0, 0.0).astype(F32)


def _mixer_params(tl, norm_g, w_in, conv_w, conv_b, dt_bias, a_log, d_skip, ssd_norm_g, w_out):
    d, in_cols = w_in.shape
    n_heads = dt_bias.shape[0]
    ssd_w = n_heads * SSD_HEAD_DIM
    conv_dim = conv_w.shape[1]
    ret_w = (in_cols - ssd_w - conv_dim - n_heads) // 4
    rd = ret_w // RET_HEADS
    o0 = ssd_w
    o1 = o0 + conv_dim
    o2 = o1 + n_heads
    perm = (jnp.arange(RET_HEADS)[:, None] * rd
            + jnp.concatenate([jnp.arange(0, rd, 2), jnp.arange(1, rd, 2)])[None, :]).reshape(-1)
    wq = w_in[:, o2:o2 + ret_w][:, perm]
    wk = w_in[:, o2 + ret_w:o2 + 2 * ret_w][:, perm]
    wdt = w_in[:, o1:o2]
    intra, xi, zeta, cdec = _retention_constants()
    tri = _block_tril(tl)
    return {
        "inv_freq": (1.0 / (ROPE_BASE ** jnp.linspace(0.0, 1.0, rd // 2, dtype=F32)))[None, :],
        "gmix": norm_g[None, :].astype(F32),
        "wz": _pack_rows(w_in[:, :o0]),
        "wxbc": _pack_rows(w_in[:, o0:o1]),
        "wdt": wdt.astype(BF16),
        "wdtt": wdt.T.astype(BF16),
        "wq": _pack_rows(wq),
        "wk": _pack_rows(wk),
        "wv": _pack_rows(w_in[:, o2 + 2 * ret_w:o2 + 3 * ret_w]),
        "wg": _pack_rows(w_in[:, o2 + 3 * ret_w:]),
        "conv_w": conv_w.astype(F32),
        "conv_b": conv_b[None, :].astype(F32),
        "dtb_row": dt_bias[None, :].astype(F32),
        "dtb_col": dt_bias[:, None].astype(F32),
        "alog_row": a_log[None, :].astype(F32),
        "alog_col": a_log[:, None].astype(F32),
        "dexp": jnp.repeat(d_skip.astype(F32), SSD_HEAD_DIM)[None, :],
        "gssd": ssd_norm_g[None, :].astype(F32),
        "expand": jnp.repeat(jnp.eye(n_heads, dtype=F32), SSD_HEAD_DIM, axis=1).astype(BF16),
        "ltri": tri.astype(BF16),
        "utri": tri.T.astype(BF16),
        "intra": intra, "xi": xi, "zeta": zeta, "cdec": cdec,
        "wout": _pack_rows(w_out),
    }


def kernel(x, mem, positions, norm_mix_g, w_in, conv_w, conv_b, dt_bias, a_log, d_skip, ssd_norm_g, w_out, norm_xattn_g, w_xq, w_xk, w_xv, w_xo, norm_ffn_g, w_gate, w_up, w_down, norm_final_g):
    depth = w_in.shape[0]
    pos = positions.astype(F32)[:, :, None]
    tl = _mixer_tile(x.shape[1])
    h = x
    for i in range(depth):
        mp = _mixer_params(tl, norm_mix_g[i], w_in[i], conv_w[i], conv_b[i], dt_bias[i], a_log[i],
                           d_skip[i], ssd_norm_g[i], w_out[i])
        h = _mixer_call(h, pos, mp)
        k, v = _kv_call(mem, _pack_rows(w_xk[i]), _pack_rows(w_xv[i]))
        xp = {
            "gx": norm_xattn_g[i][None, :].astype(F32), "wxq": _pack_rows(w_xq[i]),
            "wxo": _pack_rows(w_xo[i]), "gf": norm_ffn_g[i][None, :].astype(F32),
            "wgate": _pack_rows(w_gate[i]), "wup": _pack_rows(w_up[i]),
            "wdown": _pack_rows(w_down[i]), "gfin": norm_final_g[None, :].astype(F32),
        }
        h = _xattn_ffn_call(h, k, v, xp, final_norm=(i == depth - 1))
    return h
```
